```python
import math
import jax
import jax.numpy as jnp
from jax import lax
import numpy as np

D_MODEL = 2048
BATCH = 4
SEQ = 2048
DEPTH = 4
DEC_BATCH = 8
DEC_SEQ = 1
PAST_LEN = 16384
PAGE_SIZE = 128

HEAD_DIM = 128
H_A = 4
DK_A = 64
DV_A = 128
GATE_RANK = 16
GATE_TAU = 16.0
H_B = 4
DK_B = 128
DV_B = 128
CONV_W = 4
CONV_CH = H_B * (2 * DK_B + DV_B)
H_C = 8
MIX_W = H_A * DV_A + H_B * DV_B + H_C * HEAD_DIM
D_FF = 5632
GLA_CHUNK = 64
GDN_CHUNK = 64
Q_BLOCK = 128
FORGET_BIAS = 7.0
EPS = 1e-6
SPLIT_SIZES = (H_A * DK_A, H_A * DK_A, H_A * DV_A, GATE_RANK, H_A * DV_A,
               CONV_CH, H_B, H_B, H_B * DV_B,
               H_C * HEAD_DIM, H_C * HEAD_DIM, H_C * HEAD_DIM, H_C)
IN_DIM = sum(SPLIT_SIZES)

kernel_name = "hymba_gla_gdn_fox_macaron_step"


def split_points():
    pts, acc = [], 0
    for s in SPLIT_SIZES[:-1]:
        acc += s
        pts.append(acc)
    return pts


def rms_norm(x, g):
    xf = x.astype(jnp.float32)
    y = xf * lax.rsqrt(jnp.mean(xf * xf, axis=-1, keepdims=True) + EPS)
    return (y * g.astype(jnp.float32)).astype(x.dtype)


def l2_norm(x):
    xf = x.astype(jnp.float32)
    return xf * lax.rsqrt(jnp.sum(xf * xf, axis=-1, keepdims=True) + EPS)


def half_ffn(x, g, w_gate, w_up, w_down):
    h = rms_norm(x, g)
    return x + 0.5 * ((jax.nn.silu(h @ w_gate) * (h @ w_up)) @ w_down)


def chunk_len(T, c):
    return c if T % c == 0 else T


def to_chunks(a, c):
    B, T, H = a.shape[:3]
    a = a.reshape((B, T // c, c, H) + a.shape[3:])
    return jnp.moveaxis(a, (1, 3), (0, 2))


def from_chunks(a):
    n, B, H, c, d = a.shape
    return jnp.moveaxis(a, (0, 2), (1, 3)).reshape(B, n * c, H, d)


def gla_scan(q, k, v, lg, s0):
    c = chunk_len(q.shape[1], GLA_CHUNK)
    causal = jnp.tril(jnp.ones((c, c), bool))

    def step(S, xs):
        qc, kc, vc, gc = xs
        b = jnp.cumsum(gc, axis=2)
        o_inter = jnp.einsum('bhtk,bhkv->bhtv', qc * jnp.exp(b), S)
        diff = b[:, :, :, None, :] - b[:, :, None, :, :]
        decay = jnp.exp(jnp.where(causal[:, :, None], diff, -jnp.inf))
        att = jnp.einsum('bhtk,bhsk,bhtsk->bhts', qc, kc, decay)
        o = o_inter + jnp.einsum('bhts,bhsv->bhtv', att, vc)
        b_last = b[:, :, -1:, :]
        S = jnp.exp(b_last[:, :, 0, :, None]) * S + jnp.einsum(
            'bhsk,bhsv->bhkv', kc * jnp.exp(b_last - b), vc)
        return S, o

    S, o = lax.scan(step, s0, (to_chunks(q, c), to_chunks(k, c), to_chunks(v, c), to_chunks(lg, c)))
    return from_chunks(o), S


def gdn_scan(q, k, v, beta, g, s0):
    c = chunk_len(q.shape[1], GDN_CHUNK)
    causal = jnp.tril(jnp.ones((c, c), bool))
    strict = jnp.tril(jnp.ones((c, c), bool), -1)
    eye = jnp.eye(c, dtype=jnp.float32)

    def step(S, xs):
        qc, kc, vc, bc, gc = xs
        G = jnp.cumsum(gc, axis=2)
        diff = G[..., :, None] - G[..., None, :]
        E = jnp.exp(jnp.where(causal, diff, -jnp.inf))
        kbeta = kc * bc[..., None]
        A = jnp.where(strict, jnp.einsum('bhtk,bhsk->bhts', kbeta, kc) * E, 0.0)
        rhs = jnp.concatenate([vc * bc[..., None], kbeta * jnp.exp(G)[..., None]], axis=-1)
        sol = lax.linalg.triangular_solve(A + eye, rhs, left_side=True, lower=True,
                                          unit_diagonal=True)
        u, w = sol[..., :DV_B], sol[..., DV_B:]
        v_new = u - jnp.einsum('bhtk,bhkv->bhtv', w, S)
        qk = jnp.einsum('bhtk,bhsk->bhts', qc, kc) * E
        o = jnp.einsum('bhtk,bhkv->bhtv', qc * jnp.exp(G)[..., None], S) + \
            jnp.einsum('bhts,bhsv->bhtv', qk, v_new)
        G_last = G[..., -1:]
        S = jnp.exp(G_last)[..., None] * S + jnp.einsum(
            'bhsk,bhsv->bhkv', kc * jnp.exp(G_last - G)[..., None], v_new)
        return S, o

    S, o = lax.scan(step, s0, (to_chunks(q, c), to_chunks(k, c), to_chunks(v, c),
                               to_chunks(beta, c), to_chunks(g, c)))
    return from_chunks(o), S


def short_conv(x, buf, w):
    T = x.shape[1]
    xp = jnp.concatenate([buf.astype(x.dtype), x], axis=1)
    y = xp[:, 0:T] * w[0]
    for i in range(1, CONV_W):
        y = y + xp[:, i:i + T] * w[i]
    return jax.nn.silu(y), xp[:, -(CONV_W - 1):]


def fox_prompt(q, k, v, logf):
    B, T, H, D = q.shape
    qb = chunk_len(T, Q_BLOCK)
    c = jnp.cumsum(logf, axis=1)
    c_k = jnp.transpose(c, (0, 2, 1))
    kf = k.astype(jnp.float32)
    kpos = jnp.arange(T)
    scale = D ** -0.5

    def block(i):
        start = i * qb
        qi = lax.dynamic_slice_in_dim(q, start, qb, axis=1).astype(jnp.float32)
        ci = jnp.transpose(lax.dynamic_slice_in_dim(c, start, qb, axis=1), (0, 2, 1))
        qpos = start + jnp.arange(qb)
        logits = jnp.einsum('bqhd,bkhd->bhqk', qi, kf) * scale + (ci[..., :, None] - c_k[..., None, :])
        logits = jnp.where(kpos[None, :] <= qpos[:, None], logits, -jnp.inf)
        p = jax.nn.softmax(logits, axis=-1)
        return jnp.einsum('bhqk,bkhd->bqhd', p, v)

    out = lax.map(block, jnp.arange(T // qb))
    return jnp.moveaxis(out, 0, 1).reshape(B, T, H, D)


def fox_sample(q, k_new, v_new, logf_new, k_past, v_past, logf_past):
    B, T, H, D = q.shape
    P = k_past.shape[1]
    L = jnp.concatenate([logf_past.astype(jnp.float32), logf_new], axis=1)
    suffix = lax.cumsum(L, axis=1, reverse=True) - L
    bias = suffix[:, None, :, :] - suffix[:, P:, None, :]
    bias = jnp.transpose(bias, (0, 3, 1, 2))
    qf = q.astype(jnp.float32)
    scale = D ** -0.5
    logits = jnp.concatenate([
        jnp.einsum('bqhd,bkhd->bhqk', qf, k_past.astype(jnp.float32)),
        jnp.einsum('bqhd,bkhd->bhqk', qf, k_new.astype(jnp.float32))], axis=-1) * scale + bias
    qpos = P + jnp.arange(T)
    kpos = jnp.arange(P + T)
    logits = jnp.where(kpos[None, :] <= qpos[:, None], logits, -jnp.inf)
    p = jax.nn.softmax(logits, axis=-1)
    return jnp.einsum('bhqk,bkhd->bqhd', p[..., :P], v_past) + \
        jnp.einsum('bhqk,bkhd->bqhd', p[..., P:], v_new)


def gather_pages(cache, page_table):
    g = cache[page_table]
    return g.reshape((g.shape[0], g.shape[1] * g.shape[2]) + g.shape[3:])


def token_mixing(h, p, gla_s0, gdn_s0, conv_buf, past):
    f32 = jnp.float32
    B, T, _ = h.shape
    proj = h @ p['w_in']
    (a_q, a_k, a_v, a_lr, a_r, b_qkv, b_beta, b_a, b_g,
     c_q, c_k, c_v, c_f) = jnp.split(proj, split_points(), axis=-1)
    qa = a_q.reshape(B, T, H_A, DK_A).astype(f32) * DK_A ** -0.5
    ka = a_k.reshape(B, T, H_A, DK_A).astype(f32)
    va = a_v.reshape(B, T, H_A, DV_A).astype(f32)
    gate = (a_lr @ p['gla_w_alpha'] + p['gla_b_alpha']).astype(f32)
    lga = jax.nn.log_sigmoid(gate).reshape(B, T, H_A, DK_A) / GATE_TAU
    oa, gla_s = gla_scan(qa, ka, va, lga, gla_s0.astype(f32))
    oa = rms_norm(oa, p['gla_norm']).reshape(B, T, H_A * DV_A) * jax.nn.silu(a_r.astype(f32))
    qkv, conv_new = short_conv(b_qkv, conv_buf, p['gdn_conv_w'])
    qb, kb, vb = jnp.split(qkv, [H_B * DK_B, 2 * H_B * DK_B], axis=-1)
    qb = l2_norm(qb.reshape(B, T, H_B, DK_B)) * DK_B ** -0.5
    kb = l2_norm(kb.reshape(B, T, H_B, DK_B))
    vb = vb.reshape(B, T, H_B, DV_B).astype(f32)
    beta = jax.nn.sigmoid(b_beta.astype(f32))
    g = -jnp.exp(p['gdn_a_log'].astype(f32)) * jax.nn.softplus(
        b_a.astype(f32) + p['gdn_dt_bias'].astype(f32))
    ob, gdn_s = gdn_scan(qb, kb, vb, beta, g, gdn_s0.astype(f32))
    ob = rms_norm(ob, p['gdn_norm']).reshape(B, T, H_B * DV_B) * jax.nn.silu(b_g.astype(f32))
    qc = rms_norm(c_q.reshape(B, T, H_C, HEAD_DIM), p['fox_q_norm'])
    kc = rms_norm(c_k.reshape(B, T, H_C, HEAD_DIM), p['fox_k_norm'])
    vc = c_v.reshape(B, T, H_C, HEAD_DIM)
    logf = jax.nn.log_sigmoid((c_f + p['fox_b_f']).astype(f32))
    if past is None:
        oc = fox_prompt(qc, kc, vc, logf)
    else:
        oc = fox_sample(qc, kc, vc, logf, past[0], past[1], past[2])
    oc = oc.reshape(B, T, H_C * HEAD_DIM)
    o = jnp.concatenate([oa.astype(h.dtype), ob.astype(h.dtype), oc.astype(h.dtype)],
                        axis=-1) @ p['w_out']
    new_state = (gla_s.astype(gla_s0.dtype), gdn_s.astype(gdn_s0.dtype), conv_new,
                 kc, vc, logf.astype(h.dtype))
    return o, new_state


def trunk(x, params, gla0, gdn0, conv0, past_fn):
    new = []
    for l in range(DEPTH):
        p = {name: w[l] for name, w in params.items()}
        x = half_ffn(x, p['ffn1_norm'], p['ffn1_w_gate'], p['ffn1_w_up'], p['ffn1_w_down'])
        past = None if past_fn is None else past_fn(l)
        o, st = token_mixing(rms_norm(x, p['mix_norm']), p, gla0[l], gdn0[l], conv0[l], past)
        x = x + o
        x = half_ffn(x, p['ffn2_norm'], p['ffn2_w_gate'], p['ffn2_w_up'], p['ffn2_w_down'])
        new.append(st)
    stacked = tuple(jnp.stack([st[i] for st in new]) for i in range(len(new[0])))
    return x, stacked


def setup_inputs(seed: int = 0) -> dict:
    key = jax.random.key(seed)
    keys = list(jax.random.split(key, 48))

    def nrm(shape, scale):
        return jax.random.normal(keys.pop(), shape, jnp.float32) * scale

    def gain(shape):
        return 1.0 + nrm(shape, 0.02)

    n_pages = PAST_LEN // PAGE_SIZE
    n_used = DEC_BATCH * n_pages
    n_pool = n_used + (n_used + 3) // 4
    x_prompt = nrm((BATCH, SEQ, D_MODEL), 1.0)
    x_sample = nrm((DEC_BATCH, DEC_SEQ, D_MODEL), 1.0)
    state_gla = nrm((DEPTH, DEC_BATCH, H_A, DK_A, DV_A), 0.5)
    state_gdn = nrm((DEPTH, DEC_BATCH, H_B, DK_B, DV_B), 0.1)
    state_conv = nrm((DEPTH, DEC_BATCH, CONV_W - 1, CONV_CH), 1.0)
    cache_k = nrm((DEPTH, n_pool, PAGE_SIZE, H_C, HEAD_DIM), 1.0)
    cache_v = nrm((DEPTH, n_pool, PAGE_SIZE, H_C, HEAD_DIM), 1.0)
    cache_logf = jax.nn.log_sigmoid(FORGET_BIAS + nrm((DEPTH, n_pool, PAGE_SIZE, H_C), 1.0))
    page_table = jax.random.permutation(keys.pop(), n_pool)[:n_used].reshape(
        DEC_BATCH, n_pages).astype(jnp.int32)
    ffn1_norm = gain((DEPTH, D_MODEL))
    ffn1_w_gate = nrm((DEPTH, D_MODEL, D_FF), D_MODEL ** -0.5)
    ffn1_w_up = nrm((DEPTH, D_MODEL, D_FF), D_MODEL ** -0.5)
    ffn1_w_down = nrm((DEPTH, D_FF, D_MODEL), D_FF ** -0.5)
    mix_norm = gain((DEPTH, D_MODEL))
    w_in = nrm((DEPTH, D_MODEL, IN_DIM), D_MODEL ** -0.5)
    gla_w_alpha = nrm((DEPTH, GATE_RANK, H_A * DK_A), GATE_RANK ** -0.5)
    gla_b_alpha = nrm((DEPTH, H_A * DK_A), 0.1)
    gla_norm = gain((DEPTH, DV_A))
    gdn_conv_w = nrm((DEPTH, CONV_W, CONV_CH), CONV_W ** -0.5)
    gdn_a_log = jnp.log(jax.random.uniform(keys.pop(), (DEPTH, H_B), jnp.float32, 1.0, 16.0))
    dt = jnp.exp(jax.random.uniform(keys.pop(), (DEPTH, H_B), jnp.float32,
                                    math.log(1e-3), math.log(1e-1)))
    gdn_dt_bias = dt + jnp.log(-jnp.expm1(-dt))
    gdn_norm = gain((DEPTH, DV_B))
    fox_b_f = FORGET_BIAS + nrm((DEPTH, H_C), 0.1)
    fox_q_norm = gain((DEPTH, HEAD_DIM))
    fox_k_norm = gain((DEPTH, HEAD_DIM))
    w_out = nrm((DEPTH, MIX_W, D_MODEL), MIX_W ** -0.5)
    ffn2_norm = gain((DEPTH, D_MODEL))
    ffn2_w_gate = nrm((DEPTH, D_MODEL, D_FF), D_MODEL ** -0.5)
    ffn2_w_up = nrm((DEPTH, D_MODEL, D_FF), D_MODEL ** -0.5)
    ffn2_w_down = nrm((DEPTH, D_FF, D_MODEL), D_FF ** -0.5)
    return {'x_prompt': x_prompt, 'x_sample': x_sample,
            'state_gla': state_gla, 'state_gdn': state_gdn, 'state_conv': state_conv,
            'cache_k': cache_k, 'cache_v': cache_v, 'cache_logf': cache_logf,
            'page_table': page_table,
            'ffn1_norm': ffn1_norm, 'ffn1_w_gate': ffn1_w_gate, 'ffn1_w_up': ffn1_w_up,
            'ffn1_w_down': ffn1_w_down, 'mix_norm': mix_norm, 'w_in': w_in,
            'gla_w_alpha': gla_w_alpha, 'gla_b_alpha': gla_b_alpha, 'gla_norm': gla_norm,
            'gdn_conv_w': gdn_conv_w, 'gdn_a_log': gdn_a_log, 'gdn_dt_bias': gdn_dt_bias,
            'gdn_norm': gdn_norm, 'fox_b_f': fox_b_f, 'fox_q_norm': fox_q_norm,
            'fox_k_norm': fox_k_norm, 'w_out': w_out,
            'ffn2_norm': ffn2_norm, 'ffn2_w_gate': ffn2_w_gate, 'ffn2_w_up': ffn2_w_up,
            'ffn2_w_down': ffn2_w_down}


def reference(x_prompt, x_sample, state_gla, state_gdn, state_conv, cache_k, cache_v,
              cache_logf, page_table, ffn1_norm, ffn1_w_gate, ffn1_w_up, ffn1_w_down,
              mix_norm, w_in, gla_w_alpha, gla_b_alpha, gla_norm, gdn_conv_w, gdn_a_log,
              gdn_dt_bias, gdn_norm, fox_b_f, fox_q_norm, fox_k_norm, w_out,
              ffn2_norm, ffn2_w_gate, ffn2_w_up, ffn2_w_down):
    params = {'ffn1_norm': ffn1_norm, 'ffn1_w_gate': ffn1_w_gate, 'ffn1_w_up': ffn1_w_up,
              'ffn1_w_down': ffn1_w_down, 'mix_norm': mix_norm, 'w_in': w_in,
              'gla_w_alpha': gla_w_alpha, 'gla_b_alpha': gla_b_alpha, 'gla_norm': gla_norm,
              'gdn_conv_w': gdn_conv_w, 'gdn_a_log': gdn_a_log, 'gdn_dt_bias': gdn_dt_bias,
              'gdn_norm': gdn_norm, 'fox_b_f': fox_b_f, 'fox_q_norm': fox_q_norm,
              'fox_k_norm': fox_k_norm, 'w_out': w_out,
              'ffn2_norm': ffn2_norm, 'ffn2_w_gate': ffn2_w_gate, 'ffn2_w_up': ffn2_w_up,
              'ffn2_w_down': ffn2_w_down}
    bp = x_prompt.shape[0]
    dt = x_prompt.dtype
    zero_gla = jnp.zeros((DEPTH, bp, H_A, DK_A, DV_A), dt)
    zero_gdn = jnp.zeros((DEPTH, bp, H_B, DK_B, DV_B), dt)
    zero_conv = jnp.zeros((DEPTH, bp, CONV_W - 1, CONV_CH), dt)
    y_prompt, (p_gla, p_gdn, p_conv, p_k, p_v, p_logf) = trunk(
        x_prompt, params, zero_gla, zero_gdn, zero_conv, None)

    def past_fn(l):
        return (gather_pages(cache_k[l], page_table), gather_pages(cache_v[l], page_table),
                gather_pages(cache_logf[l], page_table))

    y_sample, (s_gla, s_gdn, s_conv, s_k, s_v, s_logf) = trunk(
        x_sample, params, state_gla, state_gdn, state_conv, past_fn)
    return (y_prompt, y_sample, p_gla, p_gdn, p_conv, p_k, p_v, p_logf,
            s_gla, s_gdn, s_conv, s_k, s_v, s_logf)
```

```python
import functools

import jax
import jax.numpy as jnp
from jax import lax
from jax.experimental import pallas as pl
from jax.experimental.pallas import tpu as pltpu

F32 = jnp.float32
BF16 = jnp.bfloat16
HIGHEST = lax.Precision.HIGHEST

D_MODEL = 2048
DEPTH = 4
H_A, DK_A, DV_A = 4, 64, 128
GATE_RANK = 16
GATE_TAU = 16.0
H_B, DK_B, DV_B = 4, 128, 128
CONV_W = 4
CONV_CH = H_B * (2 * DK_B + DV_B)
H_C, HEAD_DIM = 8, 128
D_FF = 5632
CHUNK = 64
PAGE_SIZE = 128
EPS = 1e-6

LANES = 128
SUBLANES = 8
VMEM_LIMIT = 56 * 1024 * 1024

OFF_CQ, OFF_CK, OFF_CV = 0, 1024, 2048
OFF_BQKV = 3072
OFF_BG = 4608
OFF_AV, OFF_AR = 5120, 5632
OFF_AQ, OFF_AK = 6144, 6400
OFF_SMALL = 6656
PROJ_TN = 768
N_PROJ = 9 * PROJ_TN
SM_CF, SM_LR, SM_BETA, SM_A = 0, 8, 24, 28


def _bdot(a, b):
    return jnp.dot(a.astype(BF16), b.astype(BF16), preferred_element_type=F32)


def _bdot_nt(a, b):
    return lax.dot_general(a.astype(BF16), b.astype(BF16), (((1,), (1,)), ((), ())),
                           preferred_element_type=F32)


def _bdot_tn(a, b):
    return lax.dot_general(a.astype(BF16), b.astype(BF16), (((0,), (0,)), ((), ())),
                           preferred_element_type=F32)


def _fdot(a, b):
    return jnp.dot(a, b, preferred_element_type=F32, precision=HIGHEST)


def _sigmoid(x):
    return 1.0 / (1.0 + jnp.exp(-x))


def _silu(x):
    return x * _sigmoid(x)


def _softplus(x):
    return jnp.maximum(x, 0.0) + jnp.log1p(jnp.exp(-jnp.abs(x)))


def _log_sigmoid(x):
    return -_softplus(-x)


def _iota2(shape):
    return (lax.broadcasted_iota(jnp.int32, shape, 0), lax.broadcasted_iota(jnp.int32, shape, 1))


def _params(sem):
    return pltpu.CompilerParams(dimension_semantics=sem, vmem_limit_bytes=VMEM_LIMIT)


def _ffn_body(x_ref, g_ref, wg_ref, wu_ref, wd_ref, o_ref, h_ref, acc_ref):
    f = pl.program_id(1)

    @pl.when(f == 0)
    def _():
        x = x_ref[...]
        ms = jnp.mean(x * x, axis=-1, keepdims=True)
        h_ref[...] = (x * lax.rsqrt(ms + EPS) * g_ref[...]).astype(BF16)
        acc_ref[...] = jnp.zeros_like(acc_ref)

    h = h_ref[...]
    a = jnp.dot(h, wg_ref[...], preferred_element_type=F32)
    u = jnp.dot(h, wu_ref[...], preferred_element_type=F32)
    act = (_silu(a) * u).astype(BF16)
    acc_ref[...] += jnp.dot(act, wd_ref[...], preferred_element_type=F32)

    @pl.when(f == pl.num_programs(1) - 1)
    def _():
        o_ref[...] = x_ref[...] + 0.5 * acc_ref[...]


def _ffn(x, g, wg, wu, wd, *, tm=512, tf=512):
    m, d = x.shape
    f = wg.shape[1]
    tm = min(tm, m)
    tf = min(tf, f)
    assert m % tm == 0 and f % tf == 0
    return pl.pallas_call(
        _ffn_body,
        grid=(m // tm, f // tf),
        in_specs=[
            pl.BlockSpec((tm, d), lambda i, j: (i, 0)),
            pl.BlockSpec((1, d), lambda i, j: (0, 0)),
            pl.BlockSpec((d, tf), lambda i, j: (0, j)),
            pl.BlockSpec((d, tf), lambda i, j: (0, j)),
            pl.BlockSpec((tf, d), lambda i, j: (j, 0)),
        ],
        out_specs=pl.BlockSpec((tm, d), lambda i, j: (i, 0)),
        out_shape=jax.ShapeDtypeStruct((m, d), F32),
        scratch_shapes=[pltpu.VMEM((tm, d), BF16), pltpu.VMEM((tm, d), F32)],
        compiler_params=_params(("parallel", "arbitrary")),
        name="ffn",
    )(x, g.reshape(1, d), wg, wu, wd)


def _proj_body(x_ref, g_ref, w_ref, o_ref, h_ref):
    @pl.when(pl.program_id(1) == 0)
    def _():
        x = x_ref[...]
        ms = jnp.mean(x * x, axis=-1, keepdims=True)
        h_ref[...] = (x * lax.rsqrt(ms + EPS) * g_ref[...]).astype(BF16)

    o_ref[...] = jnp.dot(h_ref[...], w_ref[...], preferred_element_type=F32)


def _proj(x, g, w, *, tm=1024, tn=PROJ_TN):
    m, d = x.shape
    n = w.shape[1]
    tm = min(tm, m)
    assert m % tm == 0 and n % tn == 0
    return pl.pallas_call(
        _proj_body,
        grid=(m // tm, n // tn),
        in_specs=[
            pl.BlockSpec((tm, d), lambda i, j: (i, 0)),
            pl.BlockSpec((1, d), lambda i, j: (0, 0)),
            pl.BlockSpec((d, tn), lambda i, j: (0, j)),
        ],
        out_specs=pl.BlockSpec((tm, tn), lambda i, j: (i, j)),
        out_shape=jax.ShapeDtypeStruct((m, n), F32),
        scratch_shapes=[pltpu.VMEM((tm, d), BF16)],
        compiler_params=_params(("parallel", "arbitrary")),
        name="proj",
    )(x, g.reshape(1, d), w)


def _out_body(x_ref, oa_ref, ob_ref, oc_ref, w_ref, o_ref):
    na = oa_ref.shape[1]
    nb = ob_ref.shape[1]
    nc = oc_ref.shape[1]
    y = _bdot(oa_ref[...], w_ref[0:na, :])
    y = y + _bdot(ob_ref[...], w_ref[na:na + nb, :])
    y = y + _bdot(oc_ref[...], w_ref[na + nb:na + nb + nc, :])
    o_ref[...] = x_ref[...] + y


def _out_proj(x, oa, ob, oc, w, *, tm=512):
    m, d = x.shape
    tm = min(tm, m)
    assert m % tm == 0
    row = lambda i: (i, 0)
    return pl.pallas_call(
        _out_body,
        grid=(m // tm,),
        in_specs=[
            pl.BlockSpec((tm, d), row),
            pl.BlockSpec((tm, oa.shape[1]), row),
            pl.BlockSpec((tm, ob.shape[1]), row),
            pl.BlockSpec((tm, oc.shape[1]), row),
            pl.BlockSpec(w.shape, lambda i: (0, 0)),
        ],
        out_specs=pl.BlockSpec((tm, d), row),
        out_shape=jax.ShapeDtypeStruct((m, d), F32),
        compiler_params=_params(("parallel",)),
        name="out_proj",
    )(x, oa, ob, oc, w)


def _gla_body(q_ref, k_ref, v_ref, r_ref, sm_ref, wa_ref, ba_ref, gn_ref, s0_ref,
              o_ref, so_ref, s_ref, b_ref, *, t_valid):
    c = pl.program_id(1)
    C = CHUNK
    scale = DK_A ** -0.5

    @pl.when(c == 0)
    def _():
        s_ref[...] = s0_ref[0]

    row, col = _iota2((C, C))
    tril = (row >= col).astype(F32)
    gate = _bdot(sm_ref[0], wa_ref[...]) + ba_ref[...]
    lg = _log_sigmoid(gate) * (1.0 / GATE_TAU)
    if t_valid < C:
        lg = jnp.where(lax.broadcasted_iota(jnp.int32, lg.shape, 0) < t_valid, lg, 0.0)
    b = _fdot(tril, lg)
    b_ref[...] = b

    q = q_ref[0]
    k = k_ref[0]
    v = v_ref[0]
    r = r_ref[0]
    lane = lax.broadcasted_iota(jnp.int32, (C, LANES), 1)
    srow = lax.broadcasted_iota(jnp.int32, (C, LANES), 0)
    lo_half = lane < DK_A
    t_of_lane = lane & (DK_A - 1)

    att_t = []
    for hp in range(H_A // 2):
        sl = slice(LANES * hp, LANES * (hp + 1))
        k2 = k[:, sl]
        b2 = b[:, sl]

        def body(t8, acc, sl=sl, k2=k2, b2=b2):
            base = pl.multiple_of(t8 * SUBLANES, SUBLANES)
            q8 = q_ref[0, pl.ds(base, SUBLANES), sl] * scale
            b8 = b_ref[pl.ds(base, SUBLANES), sl]
            for r in range(SUBLANES):
                p = k2 * q8[r:r + 1, :] * jnp.exp(jnp.minimum(b8[r:r + 1, :] - b2, 0.0))
                p_lo = jnp.where(lo_half, p, 0.0)
                lo = jnp.sum(p_lo, axis=-1, keepdims=True)
                hi = jnp.sum(p - p_lo, axis=-1, keepdims=True)
                colv = jnp.where(lo_half, lo, hi)
                acc = jnp.where(t_of_lane == base + r, colv, acc)
            return acc

        a2 = lax.fori_loop(0, C // SUBLANES, body, jnp.zeros((C, LANES), F32))
        a2 = jnp.where(srow <= t_of_lane, a2, 0.0)
        att_t.append(a2[:, :DK_A])
        att_t.append(a2[:, DK_A:])

    eye = row == col
    for h in range(H_A):
        ks = slice(DK_A * h, DK_A * (h + 1))
        vs = slice(DV_A * h, DV_A * (h + 1))
        bh = b[:, ks]
        qh = q[:, ks] * scale
        kh = k[:, ks]
        vh = v[:, vs]
        s_h = s_ref[h]
        o = _bdot(qh * jnp.exp(bh), s_h) + _bdot_tn(att_t[h], vh)
        bl = bh[C - 1:C, :]
        kdec = kh * jnp.exp(bl - bh)
        dg = jnp.where(eye, jnp.exp(bl), 0.0)
        s_ref[h] = _fdot(dg, s_h) + _bdot_tn(kdec, vh)
        ms = jnp.mean(o * o, axis=-1, keepdims=True)
        y = o * lax.rsqrt(ms + EPS) * gn_ref[...]
        o_ref[0, :, vs] = y * _silu(r[:, vs])

    @pl.when(c == pl.num_programs(1) - 1)
    def _():
        so_ref[0] = s_ref[...]


def _gla(p3, w_alpha_ext, b_alpha, g_norm, s0, *, t_valid):
    bsz, t, _ = p3.shape
    assert t % CHUNK == 0
    nc = t // CHUNK
    assert t_valid == t or nc == 1
    hk = H_A * DK_A
    hv = H_A * DV_A

    def col(width, off):
        return pl.BlockSpec((1, CHUNK, width), lambda b, c: (b, c, off // width))

    const2 = lambda b, c: (0, 0)
    return pl.pallas_call(
        functools.partial(_gla_body, t_valid=t_valid),
        grid=(bsz, nc),
        in_specs=[
            col(hk, OFF_AQ), col(hk, OFF_AK), col(hv, OFF_AV), col(hv, OFF_AR),
            col(LANES, OFF_SMALL),
            pl.BlockSpec((LANES, hk), const2),
            pl.BlockSpec((1, hk), const2),
            pl.BlockSpec((1, DV_A), const2),
            pl.BlockSpec((1, H_A, DK_A, DV_A), lambda b, c: (b, 0, 0, 0)),
        ],
        out_specs=[
            pl.BlockSpec((1, CHUNK, hv), lambda b, c: (b, c, 0)),
            pl.BlockSpec((1, H_A, DK_A, DV_A), lambda b, c: (b, 0, 0, 0)),
        ],
        out_shape=[
            jax.ShapeDtypeStruct((bsz, t, hv), F32),
            jax.ShapeDtypeStruct((bsz, H_A, DK_A, DV_A), F32),
        ],
        scratch_shapes=[pltpu.VMEM((H_A, DK_A, DV_A), F32), pltpu.VMEM((CHUNK, hk), F32)],
        compiler_params=_params(("parallel", "arbitrary")),
        name="gla",
    )(p3, p3, p3, p3, p3, w_alpha_ext, b_alpha.reshape(1, hk), g_norm.reshape(1, DV_A), s0)


def _unit_lower_inverse(a, row, col):
    n = a.shape[0]
    x = (row == col).astype(F32) - jnp.where((row >> 1) == (col >> 1), a, 0.0)
    s, sh = 2, 1
    while s < n:
        m = ((row >> (sh + 1)) == (col >> (sh + 1))) & (((row >> sh) & 1) == 1) & (((col >> sh) & 1) == 0)
        l21 = jnp.where(m, a, 0.0)
        x = x - _fdot(x, _fdot(l21, x))
        s, sh = s * 2, sh + 1
    return x


def _gdn_body(x_ref, g_ref, sm_ref, cw_ref, c0_ref, al_ref, dt_ref, gn_ref, s0_ref,
              o_ref, so_ref, co_ref, s_ref, xp_ref, *, t_valid):
    c = pl.program_id(1)
    C = CHUNK
    P = SUBLANES
    W1 = CONV_W - 1

    @pl.when(c == 0)
    def _():
        s_ref[...] = s0_ref[0]
        xp_ref[P - W1:P, :] = c0_ref[0]

    xp_ref[P:P + C, :] = x_ref[0]
    y = xp_ref[P - W1:P - W1 + C, :] * cw_ref[0:1, :]
    for i in range(1, CONV_W):
        y = y + xp_ref[P - W1 + i:P - W1 + i + C, :] * cw_ref[i:i + 1, :]
    qkv = _silu(y)

    tv = t_valid - (t_valid - 1) // C * C
    @pl.when(c == pl.num_programs(1) - 1)
    def _():
        co_ref[0] = xp_ref[P - W1 + tv:P + tv, :]

    xp_ref[P - W1:P, :] = xp_ref[P - W1 + C:P + C, :]

    sm = sm_ref[0]
    beta_all = _sigmoid(sm)
    g_all = -jnp.exp(al_ref[...]) * _softplus(sm + dt_ref[...])
    if t_valid < C:
        valid = lax.broadcasted_iota(jnp.int32, g_all.shape, 0) < t_valid
        g_all = jnp.where(valid, g_all, 0.0)
        beta_all = jnp.where(valid, beta_all, 0.0)
    row, col = _iota2((C, C))
    tril = (row >= col).astype(F32)
    strict = row > col
    cum_all = _fdot(tril, g_all)

    hq = H_B * DK_B
    for h in range(H_B):
        qh = qkv[:, DK_B * h:DK_B * (h + 1)]
        kh = qkv[:, hq + DK_B * h:hq + DK_B * (h + 1)]
        vh = qkv[:, 2 * hq + DV_B * h:2 * hq + DV_B * (h + 1)]
        qh = qh * lax.rsqrt(jnp.sum(qh * qh, axis=-1, keepdims=True) + EPS) * (DK_B ** -0.5)
        kh = kh * lax.rsqrt(jnp.sum(kh * kh, axis=-1, keepdims=True) + EPS)
        beta = beta_all[:, SM_BETA + h:SM_BETA + h + 1]
        gcol = g_all[:, SM_A + h:SM_A + h + 1]
        cum = cum_all[:, SM_A + h:SM_A + h + 1]
        diff = _fdot(tril, jnp.where(strict, gcol, 0.0))
        e = jnp.where(row >= col, jnp.exp(diff), 0.0)
        kbeta = kh * beta
        a = jnp.where(strict, _bdot_nt(kbeta, kh) * e, 0.0)
        tinv = _unit_lower_inverse(a, row, col)
        ecum = jnp.exp(cum)
        rhs = jnp.concatenate([vh * beta, kbeta * ecum], axis=-1)
        sol = _fdot(tinv, rhs)
        u = sol[:, :DV_B]
        w = sol[:, DV_B:]
        s_h = s_ref[h]
        v_new = u - _bdot(w, s_h)
        qk = _bdot_nt(qh, kh) * e
        o = _bdot(qh * ecum, s_h) + _bdot(qk, v_new)
        cum_last = cum[C - 1:C, :]
        s_ref[h] = jnp.exp(cum_last) * s_h + _bdot_tn(kh * jnp.exp(cum_last - cum), v_new)
        ms = jnp.mean(o * o, axis=-1, keepdims=True)
        yo = o * lax.rsqrt(ms + EPS) * gn_ref[...]
        vs = slice(DV_B * h, DV_B * (h + 1))
        o_ref[0, :, vs] = yo * _silu(g_ref[0, :, vs])

    @pl.when(c == pl.num_programs(1) - 1)
    def _():
        so_ref[0] = s_ref[...]


def _gdn(p3, conv_w, conv0, a_log_pad, dt_pad, g_norm, s0, *, t_valid):
    bsz, t, _ = p3.shape
    assert t % CHUNK == 0
    nc = t // CHUNK
    assert t_valid == t or nc == 1
    hv = H_B * DV_B

    def col(width, off):
        return pl.BlockSpec((1, CHUNK, width), lambda b, c: (b, c, off // width))

    const2 = lambda b, c: (0, 0)
    per_b3 = lambda b, c: (b, 0, 0)
    per_b4 = lambda b, c: (b, 0, 0, 0)
    return pl.pallas_call(
        functools.partial(_gdn_body, t_valid=t_valid),
        grid=(bsz, nc),
        in_specs=[
            col(CONV_CH, OFF_BQKV), col(hv, OFF_BG), col(LANES, OFF_SMALL),
            pl.BlockSpec((CONV_W, CONV_CH), const2),
            pl.BlockSpec((1, CONV_W - 1, CONV_CH), per_b3),
            pl.BlockSpec((1, LANES), const2),
            pl.BlockSpec((1, LANES), const2),
            pl.BlockSpec((1, DV_B), const2),
            pl.BlockSpec((1, H_B, DK_B, DV_B), per_b4),
        ],
        out_specs=[
            pl.BlockSpec((1, CHUNK, hv), lambda b, c: (b, c, 0)),
            pl.BlockSpec((1, H_B, DK_B, DV_B), per_b4),
            pl.BlockSpec((1, CONV_W - 1, CONV_CH), per_b3),
        ],
        out_shape=[
            jax.ShapeDtypeStruct((bsz, t, hv), F32),
            jax.ShapeDtypeStruct((bsz, H_B, DK_B, DV_B), F32),
            jax.ShapeDtypeStruct((bsz, CONV_W - 1, CONV_CH), F32),
        ],
        scratch_shapes=[pltpu.VMEM((H_B, DK_B, DV_B), F32),
                        pltpu.VMEM((SUBLANES + CHUNK, CONV_CH), F32)],
        compiler_params=_params(("parallel", "arbitrary")),
        name="gdn",
    )(p3, p3, p3, conv_w, conv0, a_log_pad, dt_pad, g_norm.reshape(1, DV_B), s0)


def _foxprep_body(q_ref, k_ref, v_ref, sm_ref, qg_ref, kg_ref, bf_ref,
                  qn_ref, kn_ref, vc_ref, lf_ref, cs_ref, carry_ref):
    i = pl.program_id(1)
    tp = q_ref.shape[1]

    @pl.when(i == 0)
    def _():
        carry_ref[...] = jnp.zeros_like(carry_ref)

    for h in range(H_C):
        sl = slice(HEAD_DIM * h, HEAD_DIM * (h + 1))
        xq = q_ref[0, :, sl]
        qn_ref[0, :, sl] = (xq * lax.rsqrt(jnp.mean(xq * xq, axis=-1, keepdims=True) + EPS)
                            * qg_ref[...]) * (HEAD_DIM ** -0.5)
        xk = k_ref[0, :, sl]
        kn_ref[0, :, sl] = xk * lax.rsqrt(jnp.mean(xk * xk, axis=-1, keepdims=True) + EPS) * kg_ref[...]
    vc_ref[0] = v_ref[0]
    lf = _log_sigmoid(sm_ref[0] + bf_ref[...])
    lf_ref[0] = lf[:, SM_CF:SM_CF + H_C]
    row, col = _iota2((tp, tp))
    cum = _fdot((row >= col).astype(F32), lf) + carry_ref[...]
    cs_ref[0] = cum[:, SM_CF:SM_CF + H_C]
    carry_ref[...] = cum[tp - 1:tp, :]


def _fox_prep(p3, q_norm, k_norm, bf_pad, *, tp=256):
    bsz, t, _ = p3.shape
    tp = min(tp, t)
    assert t % tp == 0
    w = H_C * HEAD_DIM

    def col(width, off):
        return pl.BlockSpec((1, tp, width), lambda b, i: (b, i, off // width))

    const2 = lambda b, i: (0, 0)
    blk = lambda b, i: (b, i, 0)
    return pl.pallas_call(
        _foxprep_body,
        grid=(bsz, t // tp),
        in_specs=[
            col(w, OFF_CQ), col(w, OFF_CK), col(w, OFF_CV), col(LANES, OFF_SMALL),
            pl.BlockSpec((1, HEAD_DIM), const2),
            pl.BlockSpec((1, HEAD_DIM), const2),
            pl.BlockSpec((1, LANES), const2),
        ],
        out_specs=[
            pl.BlockSpec((1, tp, w), blk), pl.BlockSpec((1, tp, w), blk), pl.BlockSpec((1, tp, w), blk),
            pl.BlockSpec((1, tp, H_C), blk), pl.BlockSpec((1, tp, H_C), blk),
        ],
        out_shape=[
            jax.ShapeDtypeStruct((bsz, t, w), F32), jax.ShapeDtypeStruct((bsz, t, w), F32),
            jax.ShapeDtypeStruct((bsz, t, w), F32),
            jax.ShapeDtypeStruct((bsz, t, H_C), F32), jax.ShapeDtypeStruct((bsz, t, H_C), F32),
        ],
        scratch_shapes=[pltpu.VMEM((1, LANES), F32)],
        compiler_params=_params(("parallel", "arbitrary")),
        name="fox_prep",
    )(p3, p3, p3, p3, q_norm.reshape(1, HEAD_DIM), k_norm.reshape(1, HEAD_DIM), bf_pad)


def _flash_body(q_ref, k_ref, v_ref, cc_ref, cr_ref, o_ref, *, tq):
    qi = pl.program_id(2)
    q = q_ref[0].astype(BF16)
    cc = cc_ref[0, 0]
    row, col = _iota2((tq, tq))

    def step(j, carry, diagonal):
        m, l, acc = carry
        start = pl.multiple_of(j * tq, tq)
        k = k_ref[0, pl.ds(start, tq), :].astype(BF16)
        v = v_ref[0, pl.ds(start, tq), :].astype(BF16)
        s = _bdot_nt(q, k) + (cc - cr_ref[0, 0, j])
        if diagonal:
            s = jnp.where(row >= col, s, -jnp.inf)
        m_new = jnp.maximum(m, jnp.max(s, axis=-1, keepdims=True))
        alpha = jnp.exp(m - m_new)
        p = jnp.exp(s - m_new)
        l = alpha * l + jnp.sum(p, axis=-1, keepdims=True)
        acc = alpha * acc + jnp.dot(p.astype(BF16), v, preferred_element_type=F32)
        return m_new, l, acc

    init = (jnp.full((tq, 1), -jnp.inf, F32), jnp.zeros((tq, 1), F32), jnp.zeros((tq, HEAD_DIM), F32))
    carry = lax.fori_loop(0, qi, lambda j, cr: step(j, cr, False), init)
    _, l, acc = step(qi, carry, True)
    o_ref[0] = acc / l


def _fox_flash(qn, kn, vc, cs, *, tq=512):
    bsz, t, w = qn.shape
    tq = min(tq, t)
    assert t % tq == 0
    nq = t // tq
    cs_t = jnp.transpose(cs, (0, 2, 1))
    c_col = cs_t[..., None]
    c_row = cs_t.reshape(bsz, H_C, nq, 1, tq)
    return pl.pallas_call(
        functools.partial(_flash_body, tq=tq),
        grid=(bsz, H_C, nq),
        in_specs=[
            pl.BlockSpec((1, tq, HEAD_DIM), lambda b, h, i: (b, i, h)),
            pl.BlockSpec((1, t, HEAD_DIM), lambda b, h, i: (b, 0, h)),
            pl.BlockSpec((1, t, HEAD_DIM), lambda b, h, i: (b, 0, h)),
            pl.BlockSpec((1, 1, tq, 1), lambda b, h, i: (b, h, i, 0)),
            pl.BlockSpec((1, 1, nq, 1, tq), lambda b, h, i: (b, h, 0, 0, 0)),
        ],
        out_specs=pl.BlockSpec((1, tq, HEAD_DIM), lambda b, h, i: (b, i, h)),
        out_shape=jax.ShapeDtypeStruct((bsz, t, w), F32),
        compiler_params=_params(("parallel", "parallel", "arbitrary")),
        name="fox_flash",
    )(qn, kn, vc, c_col, c_row)


def _bias_body(pt_ref, lf_ref, lnew_ref, o_ref, carry_ref):
    @pl.when(pl.program_id(1) == 0)
    def _():
        carry_ref[...] = lnew_ref[0]

    lf = lf_ref[...]
    row, col = _iota2((PAGE_SIZE, PAGE_SIZE))
    later = (row > col).astype(F32)
    o_ref[0] = _fdot(lf, later) + carry_ref[...]
    carry_ref[...] += jnp.sum(lf, axis=-1, keepdims=True)


def _forget_bias(page_table, cache_logf_t, layer, logf_new):
    bsz, n_pages = page_table.shape
    lnew = jnp.broadcast_to(logf_new[:, :, None], (bsz, H_C, LANES))
    grid_spec = pltpu.PrefetchScalarGridSpec(
        num_scalar_prefetch=1,
        grid=(bsz, n_pages),
        in_specs=[
            pl.BlockSpec((None, None, H_C, PAGE_SIZE),
                         lambda b, j, pt: (layer, pt[b, n_pages - 1 - j], 0, 0)),
            pl.BlockSpec((1, H_C, LANES), lambda b, j, pt: (b, 0, 0)),
        ],
        out_specs=pl.BlockSpec((1, H_C, PAGE_SIZE), lambda b, j, pt: (b, 0, n_pages - 1 - j)),
        scratch_shapes=[pltpu.VMEM((H_C, LANES), F32)],
    )
    return pl.pallas_call(
        _bias_body,
        grid_spec=grid_spec,
        out_shape=jax.ShapeDtypeStruct((bsz, H_C, n_pages * PAGE_SIZE), F32),
        compiler_params=_params(("parallel", "arbitrary")),
        name="forget_bias",
    )(page_table, cache_logf_t, lnew)


def _decattn_body(pt_ref, q_ref, kn_ref, vn_ref, bias_ref, *refs, group):
    k_refs = refs[:group]
    v_refs = refs[group:2 * group]
    o_ref, m_ref, l_ref, acc_ref = refs[2 * group:]
    i = pl.program_id(1)
    q = q_ref[0]

    @pl.when(i == 0)
    def _():
        s_new = jnp.sum(q * kn_ref[0], axis=-1, keepdims=True)
        m_ref[...] = jnp.broadcast_to(s_new, m_ref.shape)
        l_ref[...] = jnp.ones_like(l_ref)
        acc_ref[...] = vn_ref[0]

    shape3 = (PAGE_SIZE, H_C, LANES)
    pos_is_lane = lax.broadcasted_iota(jnp.int32, shape3, 0) == lax.broadcasted_iota(jnp.int32, shape3, 2)
    m = m_ref[:, 0:1]
    l = l_ref[:, 0:1]
    acc = acc_ref[...]
    for g in range(group):
        k = k_refs[g][...]
        v = v_refs[g][...]
        s = jnp.sum(k * q[None], axis=-1, keepdims=True)
        b2 = bias_ref[0, :, PAGE_SIZE * g:PAGE_SIZE * (g + 1)]
        s = s + jnp.sum(jnp.where(pos_is_lane, b2[None], 0.0), axis=-1, keepdims=True)
        m_new = jnp.maximum(m, jnp.max(s, axis=0))
        alpha = jnp.exp(m - m_new)
        p = jnp.exp(s - m_new[None])
        l = alpha * l + jnp.sum(p, axis=0)
        acc = alpha * acc + jnp.sum(p * v, axis=0)
        m = m_new
    m_ref[...] = jnp.broadcast_to(m, m_ref.shape)
    l_ref[...] = jnp.broadcast_to(l, l_ref.shape)
    acc_ref[...] = acc

    @pl.when(i == pl.num_programs(1) - 1)
    def _():
        o_ref[0] = acc / l


def _decode_attn(page_table, q3, kn3, vn3, bias, cache_k, cache_v, layer, *, group=4):
    bsz, n_pages = page_table.shape
    assert n_pages % group == 0
    hd = (1, H_C, HEAD_DIM)
    per_b = lambda b, i, pt: (b, 0, 0)

    def page_spec(g):
        return pl.BlockSpec((None, None, PAGE_SIZE, H_C, HEAD_DIM),
                            lambda b, i, pt: (layer, pt[b, i * group + g], 0, 0, 0))

    grid_spec = pltpu.PrefetchScalarGridSpec(
        num_scalar_prefetch=1,
        grid=(bsz, n_pages // group),
        in_specs=[
            pl.BlockSpec(hd, per_b), pl.BlockSpec(hd, per_b), pl.BlockSpec(hd, per_b),
            pl.BlockSpec((1, H_C, group * PAGE_SIZE), lambda b, i, pt: (b, 0, i)),
        ] + [page_spec(g) for g in range(group)] + [page_spec(g) for g in range(group)],
        out_specs=pl.BlockSpec(hd, per_b),
        scratch_shapes=[pltpu.VMEM((H_C, LANES), F32), pltpu.VMEM((H_C, LANES), F32),
                        pltpu.VMEM((H_C, HEAD_DIM), F32)],
    )
    return pl.pallas_call(
        functools.partial(_decattn_body, group=group),
        grid_spec=grid_spec,
        out_shape=jax.ShapeDtypeStruct((bsz, H_C, HEAD_DIM), F32),
        compiler_params=_params(("parallel", "arbitrary")),
        name="decode_attn",
    )(page_table, q3, kn3, vn3, bias, *([cache_k] * group), *([cache_v] * group))


def _prep_weights(w):
    bf = lambda a: a.astype(BF16)
    w_in = w['w_in']
    sizes = (H_A * DK_A, H_A * DK_A, H_A * DV_A, GATE_RANK, H_A * DV_A, CONV_CH, H_B, H_B,
             H_B * DV_B, H_C * HEAD_DIM, H_C * HEAD_DIM, H_C * HEAD_DIM, H_C)
    offs = [0]
    for s in sizes:
        offs.append(offs[-1] + s)
    seg = lambda i: w_in[:, :, offs[i]:offs[i + 1]]
    (a_q, a_k, a_v, a_lr, a_r, b_qkv, b_beta, b_a, b_g, c_q, c_k, c_v, c_f) = [seg(i) for i in range(13)]
    n_used = OFF_SMALL + H_C + GATE_RANK + 2 * H_B
    pad = jnp.zeros(w_in.shape[:2] + (N_PROJ - n_used,), w_in.dtype)
    w_perm = jnp.concatenate([c_q, c_k, c_v, b_qkv, b_g, a_v, a_r, a_q, a_k,
                              c_f, a_lr, b_beta, b_a, pad], axis=-1)

    def lane_pad(a, off):
        z = jnp.zeros((DEPTH, 1, LANES), F32)
        return lax.dynamic_update_slice(z, a.astype(F32)[:, None, :], (0, 0, off))

    wa = w['gla_w_alpha']
    wa_ext = jnp.zeros((DEPTH, LANES, H_A * DK_A), F32)
    wa_ext = lax.dynamic_update_slice(wa_ext, wa.astype(F32), (0, SM_LR, 0))
    return {
        'ffn1_wg': bf(w['ffn1_w_gate']), 'ffn1_wu': bf(w['ffn1_w_up']), 'ffn1_wd': bf(w['ffn1_w_down']),
        'ffn2_wg': bf(w['ffn2_w_gate']), 'ffn2_wu': bf(w['ffn2_w_up']), 'ffn2_wd': bf(w['ffn2_w_down']),
        'w_in': bf(w_perm), 'w_out': bf(w['w_out']),
        'wa_ext': bf(wa_ext),
        'a_log': lane_pad(w['gdn_a_log'], SM_A), 'dt_bias': lane_pad(w['gdn_dt_bias'], SM_A),
        'b_f': lane_pad(w['fox_b_f'], SM_CF),
    }


def _trunk(x, w, pw, gla0, gdn0, conv0, past):
    bsz, t, d = x.shape
    m = bsz * t
    x2 = x.reshape(m, d)
    decode = past is not None
    new = []
    for l in range(DEPTH):
        x2 = _ffn(x2, w['ffn1_norm'][l], pw['ffn1_wg'][l], pw['ffn1_wu'][l], pw['ffn1_wd'][l])
        proj = _proj(x2, w['mix_norm'][l], pw['w_in'][l])
        if decode:
            p3 = jnp.pad(proj.reshape(bsz, t, N_PROJ), ((0, 0), (0, CHUNK - t), (0, 0)))
            pf = proj.reshape(1, m, N_PROJ)
        else:
            p3 = proj.reshape(bsz, t, N_PROJ)
            pf = p3
        oa, gla_s = _gla(p3, pw['wa_ext'][l], w['gla_b_alpha'][l], w['gla_norm'][l], gla0[l], t_valid=t)
        ob, gdn_s, conv_new = _gdn(p3, w['gdn_conv_w'][l], conv0[l], pw['a_log'][l], pw['dt_bias'][l],
                                   w['gdn_norm'][l], gdn0[l], t_valid=t)
        qn, kn, vc, lf, cs = _fox_prep(pf, w['fox_q_norm'][l], w['fox_k_norm'][l], pw['b_f'][l])
        if decode:
            page_table, cache_k, cache_v, cache_logf_t = past
            hd = (bsz, H_C, HEAD_DIM)
            bias = _forget_bias(page_table, cache_logf_t, l, lf.reshape(bsz, H_C))
            oc = _decode_attn(page_table, qn.reshape(hd), kn.reshape(hd), vc.reshape(hd), bias,
                              cache_k, cache_v, l).reshape(m, H_C * HEAD_DIM)
            oa2 = oa[:, :t].reshape(m, -1)
            ob2 = ob[:, :t].reshape(m, -1)
        else:
            oc = _fox_flash(qn, kn, vc, cs).reshape(m, H_C * HEAD_DIM)
            oa2 = oa.reshape(m, -1)
            ob2 = ob.reshape(m, -1)
        x2 = _out_proj(x2, oa2, ob2, oc, pw['w_out'][l])
        x2 = _ffn(x2, w['ffn2_norm'][l], pw['ffn2_wg'][l], pw['ffn2_wu'][l], pw['ffn2_wd'][l])
        new.append((gla_s, gdn_s, conv_new,
                    kn.reshape(bsz, t, H_C, HEAD_DIM), vc.reshape(bsz, t, H_C, HEAD_DIM),
                    lf.reshape(bsz, t, H_C)))
    stacked = tuple(jnp.stack([st[i] for st in new]) for i in range(6))
    return x2.reshape(bsz, t, d), stacked


def kernel(x_prompt, x_sample, state_gla, state_gdn, state_conv, cache_k, cache_v, cache_logf,
           page_table, ffn1_norm, ffn1_w_gate, ffn1_w_up, ffn1_w_down, mix_norm, w_in, gla_w_alpha,
           gla_b_alpha, gla_norm, gdn_conv_w, gdn_a_log, gdn_dt_bias, gdn_norm, fox_b_f, fox_q_norm,
           fox_k_norm, w_out, ffn2_norm, ffn2_w_gate, ffn2_w_up, ffn2_w_down):
    w = {'ffn1_norm': ffn1_norm, 'ffn1_w_gate': ffn1_w_gate, 'ffn1_w_up': ffn1_w_up,
         'ffn1_w_down': ffn1_w_down, 'mix_norm': mix_norm, 'w_in': w_in,
         'gla_w_alpha': gla_w_alpha, 'gla_b_alpha': gla_b_alpha, 'gla_norm': gla_norm,
         'gdn_conv_w': gdn_conv_w, 'gdn_a_log': gdn_a_log, 'gdn_dt_bias': gdn_dt_bias,
         'gdn_norm': gdn_norm, 'fox_b_f': fox_b_f, 'fox_q_norm': fox_q_norm,
         'fox_k_norm': fox_k_norm, 'w_out': w_out,
         'ffn2_norm': ffn2_norm, 'ffn2_w_gate': ffn2_w_gate, 'ffn2_w_up': ffn2_w_up,
         'ffn2_w_down': ffn2_w_down}
    pw = _prep_weights(w)
    bp = x_prompt.shape[0]
    zero_gla = jnp.zeros((DEPTH, bp, H_A, DK_A, DV_A), F32)
    zero_gdn = jnp.zeros((DEPTH, bp, H_B, DK_B, DV_B), F32)
    zero_conv = jnp.zeros((DEPTH, bp, CONV_W - 1, CONV_CH), F32)
    y_prompt, p_state = _trunk(x_prompt, w, pw, zero_gla, zero_gdn, zero_conv, None)
    cache_logf_t = jnp.transpose(cache_logf, (0, 1, 3, 2))
    y_sample, s_state = _trunk(x_sample, w, pw, state_gla, state_gdn, state_conv,
                               (page_table, cache_k, cache_v, cache_logf_t))
    return (y_prompt, y_sample) + p_state + s_state
```
